```python
import jax, jax.numpy as jnp
from jax import lax
import numpy as np

D_MODEL = 1024
BATCH = 8
SEQ = 8192
DEPTH = 1

GLA_HEADS = 4
GLA_DK = 64
GLA_DV = 128
GLA_GATE_RANK = 16
GLA_TAU = 16.0
GLA_CHUNK = 64
SWA_HEADS = 8
SWA_KV_HEADS = 2
SWA_HD = 64
SWA_WINDOW = 128
SWA_BLOCK = 128
ROPE_THETA = 500000.0
ROPE_DIM = SWA_HD // 4
D_FF = 2816
CONV_WIDTH = 3
EPS = 1e-6
MAX_POS_OFFSET = 4096

GLA_QK = GLA_HEADS * GLA_DK
GLA_V = GLA_HEADS * GLA_DV
SWA_Q = SWA_HEADS * SWA_HD
SWA_KV = SWA_KV_HEADS * SWA_HD
MIX_WIDTH = GLA_V + SWA_Q
IN_SPLITS = (GLA_QK, GLA_QK, GLA_V, GLA_GATE_RANK, GLA_V, SWA_Q, SWA_KV, SWA_KV)
IN_WIDTH = GLA_QK * 2 + GLA_V * 2 + GLA_GATE_RANK + SWA_Q + SWA_KV * 2

kernel_name = "hymba_gla_swa_sink_convffn_sandwich"


def rmsnorm(x, w):
    xf = x.astype(jnp.float32)
    y = xf * lax.rsqrt(jnp.mean(xf * xf, axis=-1, keepdims=True) + EPS)
    return (y * w.astype(jnp.float32)).astype(x.dtype)


def partial_rotary(x, positions):
    half = ROPE_DIM // 2
    inv_freq = ROPE_THETA ** (-jnp.arange(half, dtype=jnp.float32) * (2.0 / ROPE_DIM))
    ang = positions.astype(jnp.float32)[..., None] * inv_freq
    cos = jnp.cos(ang)[:, :, None, :]
    sin = jnp.sin(ang)[:, :, None, :]
    xr = x[..., :ROPE_DIM].astype(jnp.float32)
    x1, x2 = xr[..., :half], xr[..., half:]
    rot = jnp.concatenate([x1 * cos - x2 * sin, x2 * cos + x1 * sin], axis=-1)
    return jnp.concatenate([rot.astype(x.dtype), x[..., ROPE_DIM:]], axis=-1)


def gla_chunked(q, k, v, log_a):
    B, T, H, dk = q.shape
    dv = v.shape[-1]
    C = GLA_CHUNK
    n = T // C

    def to_chunks(t):
        return t.reshape(B, n, C, H, t.shape[-1]).transpose(1, 0, 3, 2, 4).astype(jnp.float32)

    qc = to_chunks(q * (dk ** -0.5))
    kc, vc, gc = to_chunks(k), to_chunks(v), to_chunks(log_a)
    causal = jnp.tril(jnp.ones((C, C), dtype=bool))[:, :, None]

    def step(S, inp):
        qi, ki, vi, gi = inp
        b = jnp.cumsum(gi, axis=2)
        o_inter = jnp.einsum('bhcd,bhde->bhce', qi * jnp.exp(b), S)
        diff = b[:, :, :, None, :] - b[:, :, None, :, :]
        decay = jnp.exp(jnp.where(causal, diff, -jnp.inf))
        A = jnp.einsum('bhid,bhjd,bhijd->bhij', qi, ki, decay)
        o_intra = jnp.einsum('bhij,bhje->bhie', A, vi)
        b_last = b[:, :, -1:, :]
        S_new = jnp.exp(b_last[:, :, 0, :])[..., None] * S + jnp.einsum(
            'bhcd,bhce->bhde', ki * jnp.exp(b_last - b), vi)
        return S_new, o_inter + o_intra

    S0 = jnp.zeros((B, H, dk, dv), jnp.float32)
    _, o = lax.scan(step, S0, (qc, kc, vc, gc))
    return o.transpose(1, 0, 3, 2, 4).reshape(B, T, H, dv).astype(v.dtype)


def swa_sink_attention(q, k, v, sinks):
    B, T, Hq, hd = q.shape
    Hkv = k.shape[2]
    G = Hq // Hkv
    W = SWA_BLOCK
    n = T // W
    qb = q.reshape(B, n, W, Hkv, G, hd).astype(jnp.float32)

    def with_prev(t):
        tb = t.reshape(B, n, W, Hkv, hd).astype(jnp.float32)
        prev = jnp.pad(tb, ((0, 0), (1, 0), (0, 0), (0, 0), (0, 0)))[:, :-1]
        return jnp.concatenate([prev, tb], axis=2)

    kb, vb = with_prev(k), with_prev(v)
    s = jnp.einsum('bnqhgd,bnshd->bhgnqs', qb, kb) * (hd ** -0.5)
    blk = jnp.arange(n)[:, None, None]
    qpos = blk * W + jnp.arange(W)[None, :, None]
    kpos = (blk - 1) * W + jnp.arange(2 * W)[None, None, :]
    mask = (kpos <= qpos) & (kpos > qpos - SWA_WINDOW) & (kpos >= 0)
    s = jnp.where(mask, s, -jnp.inf)
    sink = sinks.astype(jnp.float32).reshape(Hkv, G)[None, :, :, None, None]
    m = jnp.maximum(s.max(axis=-1), sink)
    p = jnp.exp(s - m[..., None])
    denom = p.sum(axis=-1) + jnp.exp(sink - m)
    o = jnp.einsum('bhgnqs,bnshd->bnqhgd', p, vb) / denom.transpose(0, 3, 4, 1, 2)[..., None]
    return o.reshape(B, T, Hq * hd).astype(q.dtype)


def causal_depthwise_conv(h, w, b):
    T = h.shape[1]
    hp = jnp.pad(h, ((0, 0), (CONV_WIDTH - 1, 0), (0, 0)))
    out = b
    for j in range(CONV_WIDTH):
        out = out + w[j] * hp[:, j:j + T]
    return out


def setup_inputs(seed: int = 0) -> dict:
    key = jax.random.key(seed)
    ks = jax.random.split(key, 18)
    nrm = jax.random.normal
    f32 = jnp.float32
    x = nrm(ks[0], (BATCH, SEQ, D_MODEL), f32)
    offs = jax.random.randint(ks[1], (BATCH, 1), 0, MAX_POS_OFFSET, dtype=jnp.int32)
    positions = (offs + jnp.arange(SEQ, dtype=jnp.int32)[None, :]).astype(jnp.int32)
    gain = lambda k_, d: 1.0 + 0.05 * nrm(k_, (DEPTH, d), f32)
    return {
        "x": x,
        "positions": positions,
        "pre_mix_norm": gain(ks[2], D_MODEL),
        "w_in": nrm(ks[3], (DEPTH, D_MODEL, IN_WIDTH), f32) * D_MODEL ** -0.5,
        "gla_gate_up": nrm(ks[4], (DEPTH, GLA_GATE_RANK, GLA_QK), f32) * GLA_GATE_RANK ** -0.5,
        "gla_gate_bias": 0.1 * nrm(ks[5], (DEPTH, GLA_QK), f32),
        "gla_out_norm": gain(ks[6], GLA_DV),
        "swa_sinks": nrm(ks[7], (DEPTH, SWA_HEADS), f32),
        "w_out": nrm(ks[8], (DEPTH, MIX_WIDTH, D_MODEL), f32) * MIX_WIDTH ** -0.5,
        "post_mix_norm": gain(ks[9], D_MODEL),
        "pre_ffn_norm": gain(ks[10], D_MODEL),
        "w_up": nrm(ks[11], (DEPTH, D_MODEL, 2 * D_FF), f32) * D_MODEL ** -0.5,
        "conv_w": nrm(ks[12], (DEPTH, CONV_WIDTH, 2 * D_FF), f32) * CONV_WIDTH ** -0.5,
        "conv_b": 0.02 * nrm(ks[13], (DEPTH, 2 * D_FF), f32),
        "w_down": nrm(ks[14], (DEPTH, D_FF, D_MODEL), f32) * D_FF ** -0.5,
        "post_ffn_norm": gain(ks[15], D_MODEL),
    }


def reference(x, positions, pre_mix_norm, w_in, gla_gate_up, gla_gate_bias, gla_out_norm,
              swa_sinks, w_out, post_mix_norm, pre_ffn_norm, w_up, conv_w, conv_b, w_down,
              post_ffn_norm):
    B, T, _ = x.shape
    split_points = np.cumsum(IN_SPLITS)[:-1].tolist()
    for l in range(DEPTH):
        h = rmsnorm(x, pre_mix_norm[l])
        proj = h @ w_in[l]
        gq, gk, gv, glr, gg, sq, sk, sv = jnp.split(proj, split_points, axis=-1)

        gate_logits = (glr @ gla_gate_up[l] + gla_gate_bias[l]).astype(jnp.float32)
        log_a = (jax.nn.log_sigmoid(gate_logits) / GLA_TAU).reshape(B, T, GLA_HEADS, GLA_DK)
        o_gla = gla_chunked(gq.reshape(B, T, GLA_HEADS, GLA_DK),
                            gk.reshape(B, T, GLA_HEADS, GLA_DK),
                            gv.reshape(B, T, GLA_HEADS, GLA_DV), log_a)
        o_gla = rmsnorm(o_gla, gla_out_norm[l]) * jax.nn.silu(gg.reshape(B, T, GLA_HEADS, GLA_DV))
        o_gla = o_gla.reshape(B, T, GLA_V)

        q = partial_rotary(sq.reshape(B, T, SWA_HEADS, SWA_HD), positions)
        k = partial_rotary(sk.reshape(B, T, SWA_KV_HEADS, SWA_HD), positions)
        v = sv.reshape(B, T, SWA_KV_HEADS, SWA_HD)
        o_swa = swa_sink_attention(q, k, v, swa_sinks[l])

        mix = jnp.concatenate([o_gla, o_swa], axis=-1) @ w_out[l]
        x = x + rmsnorm(mix, post_mix_norm[l])

        h = rmsnorm(x, pre_ffn_norm[l])
        u = causal_depthwise_conv(h @ w_up[l], conv_w[l], conv_b[l])
        val, gate = jnp.split(u, 2, axis=-1)
        y = (jax.nn.gelu(gate, approximate=True) * val) @ w_down[l]
        x = x + rmsnorm(y, post_ffn_norm[l])
    return x
```

```python
import functools

import numpy as np
import jax
import jax.numpy as jnp
from jax import lax
from jax.experimental import pallas as pl
from jax.experimental.pallas import tpu as pltpu

D_MODEL = 1024
GLA_HEADS = 4
GLA_DK = 64
GLA_DV = 128
GLA_GATE_RANK = 16
GLA_TAU = 16.0
GLA_CHUNK = 64
SWA_HEADS = 8
SWA_KV_HEADS = 2
SWA_HD = 64
SWA_BLOCK = 128
ROPE_THETA = 500000.0
ROPE_DIM = SWA_HD // 4
D_FF = 2816
CONV_WIDTH = 3
EPS = 1e-6

GLA_QK = GLA_HEADS * GLA_DK
GLA_V = GLA_HEADS * GLA_DV
SWA_Q = SWA_HEADS * SWA_HD
SWA_KV = SWA_KV_HEADS * SWA_HD
SWA_GROUP = SWA_HEADS // SWA_KV_HEADS

LANES = 128
BF16_ROWS = 16

C_GQ = 0
C_GK = C_GQ + GLA_QK
C_GV = C_GK + GLA_QK
C_GG = C_GV + GLA_V
C_SQ = C_GG + GLA_V
C_SK = C_SQ + SWA_Q
C_SV = C_SK + SWA_KV
C_LR = C_SV + SWA_KV
IN_COLS = C_LR + LANES

TM = 512
FF_TILE = 256
N_FF_TILES = D_FF // FF_TILE
VMEM_LIMIT = 56 * 1024 * 1024

_NT = (((1,), (1,)), ((), ()))
_TN = (((0,), (0,)), ((), ()))


def _rmsnorm(x, g):
    return x * lax.rsqrt(jnp.mean(x * x, axis=-1, keepdims=True) + EPS) * g


def _dot(a, b):
    return jnp.dot(a, b, preferred_element_type=jnp.float32)


def _mix_kernel(sinks_ref, x_ref, pos_ref, gpre_ref, win_ref, gup_ref, gbias_ref,
                gnorm_ref, wout_ref, gpost_ref, rope_ref, o_ref,
                st_ref, kbuf, vbuf, og_ref, os_ref):
    f32, bf16 = jnp.float32, jnp.bfloat16
    t = pl.program_id(1)

    @pl.when(t == 0)
    def _():
        st_ref[...] = jnp.zeros_like(st_ref)
        kbuf[0:SWA_BLOCK, :] = jnp.zeros((SWA_BLOCK, SWA_KV), bf16)
        vbuf[0:SWA_BLOCK, :] = jnp.zeros((SWA_BLOCK, SWA_KV), bf16)

    x = x_ref[0]
    h = _rmsnorm(x, gpre_ref[...]).astype(bf16)

    def proj(lo, width):
        return _dot(h, win_ref[:, lo:lo + width])

    glr = proj(C_LR, LANES).astype(bf16)
    z = _dot(glr, gup_ref[...]) + gbias_ref[...]
    la = (jnp.minimum(z, 0.0) - jnp.log(1.0 + jnp.exp(-jnp.abs(z)))) * (1.0 / GLA_TAU)

    blk = 256
    r = lax.broadcasted_iota(jnp.int32, (blk, blk), 0)
    c = lax.broadcasted_iota(jnp.int32, (blk, blk), 1)
    tri = jnp.where((r >= c) & ((r // GLA_CHUNK) == (c // GLA_CHUNK)), 1.0, 0.0).astype(bf16)
    la_hi = la.astype(bf16)
    rem = la - la_hi.astype(f32)
    la_mid = rem.astype(bf16)
    la_lo = (rem - la_mid.astype(f32)).astype(bf16)
    parts = []
    for i in range(TM // blk):
        rows = slice(i * blk, (i + 1) * blk)
        parts.append(_dot(tri, la_hi[rows]) + _dot(tri, la_mid[rows]) + _dot(tri, la_lo[rows]))
    b = jnp.concatenate(parts, axis=0)

    q = proj(C_GQ, GLA_QK)
    k = proj(C_GK, GLA_QK)
    v = proj(C_GV, GLA_V).astype(bf16)
    qe = (q * (GLA_DK ** -0.5) * jnp.exp(b)).astype(bf16)
    ke = (k * jnp.exp(-b)).astype(bf16)

    C = GLA_CHUNK
    lane_q = lax.broadcasted_iota(jnp.int32, (C, GLA_QK), 1) // GLA_DK
    row_a = lax.broadcasted_iota(jnp.int32, (GLA_HEADS * C, C), 0) % C
    col_a = lax.broadcasted_iota(jnp.int32, (GLA_HEADS * C, C), 1)
    causal = row_a >= col_a
    lane_s = lax.broadcasted_iota(jnp.int32, (GLA_DV, GLA_QK), 1) // GLA_DK
    for ci in range(TM // C):
        rows = slice(ci * C, (ci + 1) * C)
        bl = b[ci * C + C - 1:ci * C + C, :]
        kl = (k[rows] * jnp.exp(bl - b[rows])).astype(bf16)
        qe_c = qe[rows]
        zero = jnp.zeros_like(qe_c)
        qm = jnp.concatenate([jnp.where(lane_q == hh, qe_c, zero) for hh in range(GLA_HEADS)], axis=0)
        a = lax.dot_general(qm, ke[rows], _NT, preferred_element_type=f32)
        a = jnp.where(causal, a, 0.0).astype(bf16)
        v_c = v[rows]
        o_intra = _dot(a, v_c)
        st = st_ref[...]
        o_inter = lax.dot_general(qm, st.astype(bf16), _NT, preferred_element_type=f32)
        o_c = jnp.concatenate(
            [o_intra[hh * C:(hh + 1) * C, hh * GLA_DV:(hh + 1) * GLA_DV] + o_inter[hh * C:(hh + 1) * C]
             for hh in range(GLA_HEADS)], axis=1)
        og_ref[rows, :] = o_c
        up = lax.dot_general(v_c, kl, _TN, preferred_element_type=f32)
        new = up[0:GLA_DV]
        for hh in range(1, GLA_HEADS):
            new = jnp.where(lane_s == hh, up[hh * GLA_DV:(hh + 1) * GLA_DV], new)
        st_ref[...] = st * jnp.exp(bl) + new

    og = og_ref[...]
    gg = proj(C_GG, GLA_V)
    gn = gnorm_ref[...]
    heads = []
    for hh in range(GLA_HEADS):
        cols = slice(hh * GLA_DV, (hh + 1) * GLA_DV)
        heads.append(_rmsnorm(og[:, cols], gn))
    o_gla = jnp.concatenate(heads, axis=1) * (gg / (1.0 + jnp.exp(-gg)))

    pos = pos_ref[0].astype(f32)
    ang = pos * rope_ref[0:1, :]
    cs = jnp.cos(ang)
    sn = jnp.sin(ang)
    rc = 1.0 + rope_ref[1:2, :] * (cs - 1.0)
    rs1 = rope_ref[2:3, :] * sn
    rs2 = rope_ref[3:4, :] * sn
    half = ROPE_DIM // 2

    def rot(xg):
        return xg * rc + pltpu.roll(xg, LANES - half, 1) * rs1 + pltpu.roll(xg, half, 1) * rs2

    sq = proj(C_SQ, SWA_Q)
    scale = SWA_HD ** -0.5
    qr = [(rot(sq[:, p * LANES:(p + 1) * LANES]) * scale).astype(bf16) for p in range(SWA_GROUP)]
    kbuf[SWA_BLOCK:, :] = rot(proj(C_SK, SWA_KV)).astype(bf16)
    vbuf[SWA_BLOCK:, :] = proj(C_SV, SWA_KV).astype(bf16)

    W = SWA_BLOCK
    row_s = lax.broadcasted_iota(jnp.int32, (2 * W, 2 * W), 0) % W
    col_s = lax.broadcasted_iota(jnp.int32, (2 * W, 2 * W), 1)
    band = (col_s > row_s) & (col_s <= row_s + W)
    lane_p = lax.broadcasted_iota(jnp.int32, (W, LANES), 1)
    row_h = lax.broadcasted_iota(jnp.int32, (2 * W, 1), 0)
    for j in range(TM // W):
        kk = kbuf[j * W:j * W + 2 * W, :]
        vv = vbuf[j * W:j * W + 2 * W, :]
        mask = band
        if j == 0:
            mask = band & ((col_s >= W) | (t > 0))
        for p in range(SWA_GROUP):
            qp = qr[p][j * W:(j + 1) * W]
            zq = jnp.zeros_like(qp)
            qm = jnp.concatenate([jnp.where(lane_p < SWA_HD, qp, zq),
                                  jnp.where(lane_p >= SWA_HD, qp, zq)], axis=0)
            s = lax.dot_general(qm, kk, _NT, preferred_element_type=f32)
            s = jnp.where(mask, s, -jnp.inf)
            sink = jnp.where(row_h < W, sinks_ref[p], sinks_ref[SWA_GROUP + p])
            m = jnp.maximum(jnp.max(s, axis=1, keepdims=True), sink)
            pe = jnp.exp(s - m)
            den = jnp.sum(pe, axis=1, keepdims=True) + jnp.exp(sink - m)
            o = _dot(pe.astype(bf16), vv) * (1.0 / den)
            os_ref[j * W:(j + 1) * W, p * LANES:(p + 1) * LANES] = jnp.where(
                lane_p < SWA_HD, o[0:W], o[W:2 * W])
    kbuf[0:W, :] = kbuf[TM:TM + W, :]
    vbuf[0:W, :] = vbuf[TM:TM + W, :]

    mix = _dot(o_gla.astype(bf16), wout_ref[0:GLA_V, :]) + \
        _dot(os_ref[...].astype(bf16), wout_ref[GLA_V:GLA_V + SWA_Q, :])
    o_ref[0] = x + _rmsnorm(mix, gpost_ref[...])


def _ffn_kernel(x_ref, gpre_ref, wv_ref, wg_ref, cwv_ref, cwg_ref, cbv_ref, cbg_ref,
                wd_ref, gpost_ref, o_ref, hbuf, acc_ref):
    f32, bf16 = jnp.float32, jnp.bfloat16
    t = pl.program_id(1)
    PAD = BF16_ROWS

    @pl.when(t == 0)
    def _():
        hbuf[0:PAD, :] = jnp.zeros((PAD, D_MODEL), bf16)

    x = x_ref[0]
    hbuf[PAD:, :] = _rmsnorm(x, gpre_ref[...]).astype(bf16)
    acc_ref[...] = jnp.zeros_like(acc_ref)

    def conv(u, w, bias):
        return (bias + w[2:3] * u[PAD:] + w[1:2] * pltpu.roll(u, 1, 0)[PAD:]
                + w[0:1] * pltpu.roll(u, 2, 0)[PAD:])

    def body(j, carry):
        hext = hbuf[...]
        val = conv(_dot(hext, wv_ref[j]), cwv_ref[j], cbv_ref[j])
        gate = conv(_dot(hext, wg_ref[j]), cwg_ref[j], cbg_ref[j])
        inner = np.sqrt(2.0 / np.pi).astype(np.float32) * (gate + 0.044715 * (gate * gate * gate))
        act = 0.5 * gate * (1.0 + jnp.tanh(inner)) * val
        acc_ref[...] += _dot(act.astype(bf16), wd_ref[j])
        return carry

    lax.fori_loop(0, N_FF_TILES, body, 0)
    hbuf[0:PAD, :] = hbuf[TM:TM + PAD, :]
    o_ref[0] = x + _rmsnorm(acc_ref[...], gpost_ref[...])


def _const_spec(shape):
    nd = len(shape)
    return pl.BlockSpec(shape, lambda b, t: (0,) * nd)


def _layer(x, pos3, pre_mix_norm, w_in, gla_gate_up, gla_gate_bias, gla_out_norm, swa_sinks,
           w_out, post_mix_norm, pre_ffn_norm, w_up, conv_w, conv_b, w_down, post_ffn_norm):
    f32, bf16 = jnp.float32, jnp.bfloat16
    B, T, _ = x.shape
    grid = (B, T // TM)

    gq, gk, gv, glr, gg, sq, sk, sv = jnp.split(
        w_in, np.cumsum((GLA_QK, GLA_QK, GLA_V, GLA_GATE_RANK, GLA_V, SWA_Q, SWA_KV)).tolist(), axis=1)
    perm = np.array([[p, p + SWA_GROUP] for p in range(SWA_GROUP)]).reshape(-1)
    sq = sq.reshape(D_MODEL, SWA_HEADS, SWA_HD)[:, perm].reshape(D_MODEL, SWA_Q)
    glr = jnp.pad(glr, ((0, 0), (0, LANES - GLA_GATE_RANK)))
    win = jnp.concatenate([gq, gk, gv, gg, sq, sk, sv, glr], axis=1).astype(bf16)
    gup = jnp.pad(gla_gate_up, ((0, LANES - GLA_GATE_RANK), (0, 0))).astype(bf16)
    wo_swa = w_out[GLA_V:].reshape(SWA_HEADS, SWA_HD, D_MODEL)[perm].reshape(SWA_Q, D_MODEL)
    wout = jnp.concatenate([w_out[:GLA_V], wo_swa], axis=0).astype(bf16)

    d = np.arange(LANES) % SWA_HD
    half = ROPE_DIM // 2
    inv_freq = ROPE_THETA ** (-jnp.arange(half, dtype=f32) * (2.0 / ROPE_DIM))
    rope = jnp.zeros((8, LANES), f32)
    rope = rope.at[0].set(jnp.where(d < ROPE_DIM, inv_freq[d % half], 0.0))
    rope = rope.at[1].set(jnp.asarray(d < ROPE_DIM, f32))
    rope = rope.at[2].set(jnp.asarray(np.where(d < half, -1.0, 0.0), f32))
    rope = rope.at[3].set(jnp.asarray(np.where((d >= half) & (d < ROPE_DIM), 1.0, 0.0), f32))

    x1 = pl.pallas_call(
        _mix_kernel,
        grid=grid,
        in_specs=[
            pl.BlockSpec(memory_space=pltpu.SMEM),
            pl.BlockSpec((1, TM, D_MODEL), lambda b, t: (b, t, 0)),
            pl.BlockSpec((1, TM, 1), lambda b, t: (b, t, 0)),
            _const_spec((1, D_MODEL)),
            _const_spec((D_MODEL, IN_COLS)),
            _const_spec((LANES, GLA_QK)),
            _const_spec((1, GLA_QK)),
            _const_spec((1, GLA_DV)),
            _const_spec((GLA_V + SWA_Q, D_MODEL)),
            _const_spec((1, D_MODEL)),
            _const_spec((8, LANES)),
        ],
        out_specs=pl.BlockSpec((1, TM, D_MODEL), lambda b, t: (b, t, 0)),
        out_shape=jax.ShapeDtypeStruct(x.shape, f32),
        scratch_shapes=[
            pltpu.VMEM((GLA_DV, GLA_QK), f32),
            pltpu.VMEM((TM + SWA_BLOCK, SWA_KV), bf16),
            pltpu.VMEM((TM + SWA_BLOCK, SWA_KV), bf16),
            pltpu.VMEM((TM, GLA_V), f32),
            pltpu.VMEM((TM, SWA_Q), f32),
        ],
        compiler_params=pltpu.CompilerParams(
            dimension_semantics=("arbitrary", "arbitrary"), vmem_limit_bytes=VMEM_LIMIT),
    )(swa_sinks, x, pos3, pre_mix_norm.reshape(1, D_MODEL), win, gup,
      gla_gate_bias.reshape(1, GLA_QK), gla_out_norm.reshape(1, GLA_DV), wout,
      post_mix_norm.reshape(1, D_MODEL), rope)

    def tiles(w):
        return w.reshape(w.shape[0], N_FF_TILES, FF_TILE).transpose(1, 0, 2)

    wv = tiles(w_up[:, :D_FF]).astype(bf16)
    wg = tiles(w_up[:, D_FF:]).astype(bf16)
    cwv = tiles(conv_w[:, :D_FF])
    cwg = tiles(conv_w[:, D_FF:])
    cbv = tiles(conv_b[None, :D_FF])
    cbg = tiles(conv_b[None, D_FF:])
    wd = w_down.reshape(N_FF_TILES, FF_TILE, D_MODEL).astype(bf16)

    def const3(shape):
        return pl.BlockSpec(shape, lambda b, t: (0, 0, 0))

    out = pl.pallas_call(
        _ffn_kernel,
        grid=grid,
        in_specs=[
            pl.BlockSpec((1, TM, D_MODEL), lambda b, t: (b, t, 0)),
            _const_spec((1, D_MODEL)),
            const3((N_FF_TILES, D_MODEL, FF_TILE)),
            const3((N_FF_TILES, D_MODEL, FF_TILE)),
            const3((N_FF_TILES, CONV_WIDTH, FF_TILE)),
            const3((N_FF_TILES, CONV_WIDTH, FF_TILE)),
            const3((N_FF_TILES, 1, FF_TILE)),
            const3((N_FF_TILES, 1, FF_TILE)),
            const3((N_FF_TILES, FF_TILE, D_MODEL)),
            _const_spec((1, D_MODEL)),
        ],
        out_specs=pl.BlockSpec((1, TM, D_MODEL), lambda b, t: (b, t, 0)),
        out_shape=jax.ShapeDtypeStruct(x.shape, f32),
        scratch_shapes=[
            pltpu.VMEM((TM + BF16_ROWS, D_MODEL), bf16),
            pltpu.VMEM((TM, D_MODEL), f32),
        ],
        compiler_params=pltpu.CompilerParams(
            dimension_semantics=("arbitrary", "arbitrary"), vmem_limit_bytes=VMEM_LIMIT),
    )(x1, pre_ffn_norm.reshape(1, D_MODEL), wv, wg, cwv, cwg, cbv, cbg, wd,
      post_ffn_norm.reshape(1, D_MODEL))
    return out


def kernel(x, positions, pre_mix_norm, w_in, gla_gate_up, gla_gate_bias, gla_out_norm, swa_sinks,
           w_out, post_mix_norm, pre_ffn_norm, w_up, conv_w, conv_b, w_down, post_ffn_norm):
    B, T, _ = x.shape
    pos3 = positions.reshape(B, T, 1)
    for l in range(pre_mix_norm.shape[0]):
        x = _layer(x, pos3, pre_mix_norm[l], w_in[l], gla_gate_up[l], gla_gate_bias[l],
                   gla_out_norm[l], swa_sinks[l], w_out[l], post_mix_norm[l], pre_ffn_norm[l],
                   w_up[l], conv_w[l], conv_b[l], w_down[l], post_ffn_norm[l])
    return x
```

```python
import functools

import numpy as np
import jax
import jax.numpy as jnp
from jax import lax
from jax.experimental import pallas as pl
from jax.experimental.pallas import tpu as pltpu

D_MODEL = 1024
GLA_HEADS = 4
GLA_DK = 64
GLA_DV = 128
GLA_GATE_RANK = 16
GLA_TAU = 16.0
GLA_CHUNK = 64
SWA_HEADS = 8
SWA_KV_HEADS = 2
SWA_HD = 64
SWA_BLOCK = 128
ROPE_THETA = 500000.0
ROPE_DIM = SWA_HD // 4
D_FF = 2816
CONV_WIDTH = 3
EPS = 1e-6

GLA_QK = GLA_HEADS * GLA_DK
GLA_V = GLA_HEADS * GLA_DV
SWA_Q = SWA_HEADS * SWA_HD
SWA_KV = SWA_KV_HEADS * SWA_HD
SWA_GROUP = SWA_HEADS // SWA_KV_HEADS

LANES = 128
SUBLANES = 8
BF16_ROWS = 16

C_GQ = 0
C_GK = C_GQ + GLA_QK
C_GV = C_GK + GLA_QK
C_GG = C_GV + GLA_V
C_SQ = C_GG + GLA_V
C_SK = C_SQ + SWA_Q
C_SV = C_SK + SWA_KV
C_LR = C_SV + SWA_KV
IN_COLS = C_LR + LANES

TM = 512
FF_TILE = 256
N_FF_TILES = D_FF // FF_TILE
U_BUFS = 2
FF_ROWS = 64
GELU_C = float(np.sqrt(2.0 / np.pi))
VMEM_LIMIT = 56 * 1024 * 1024

_NT = (((1,), (1,)), ((), ()))
_TN = (((0,), (0,)), ((), ()))


def _rmsnorm(x, g):
    return x * lax.rsqrt(jnp.mean(x * x, axis=-1, keepdims=True) + EPS) * g


def _dot(a, b):
    return jnp.dot(a, b, preferred_element_type=jnp.float32)


def _mix_kernel(sinks_ref, x_ref, pos_ref, gpre_ref, win_ref, gup_ref, gbias_ref,
                gnorm_ref, wout_ref, gpost_ref, rope_ref, o_ref,
                st_ref, kbuf, vbuf, og_ref, os_ref):
    f32, bf16 = jnp.float32, jnp.bfloat16
    t = pl.program_id(1)

    @pl.when(t == 0)
    def _():
        st_ref[...] = jnp.zeros_like(st_ref)
        kbuf[0:SWA_BLOCK, :] = jnp.zeros((SWA_BLOCK, SWA_KV), bf16)
        vbuf[0:SWA_BLOCK, :] = jnp.zeros((SWA_BLOCK, SWA_KV), bf16)

    x = x_ref[0]
    h = _rmsnorm(x, gpre_ref[...]).astype(bf16)

    def proj(lo, width):
        return _dot(h, win_ref[:, lo:lo + width])

    glr = proj(C_LR, LANES).astype(bf16)
    z = _dot(glr, gup_ref[...]) + gbias_ref[...]
    la = (jnp.minimum(z, 0.0) - jnp.log(1.0 + jnp.exp(-jnp.abs(z)))) * (1.0 / GLA_TAU)

    blk = 256
    r = lax.broadcasted_iota(jnp.int32, (blk, blk), 0)
    c = lax.broadcasted_iota(jnp.int32, (blk, blk), 1)
    tri = jnp.where((r >= c) & ((r // GLA_CHUNK) == (c // GLA_CHUNK)), 1.0, 0.0).astype(bf16)
    la_hi = la.astype(bf16)
    rem = la - la_hi.astype(f32)
    la_mid = rem.astype(bf16)
    la_lo = (rem - la_mid.astype(f32)).astype(bf16)
    parts = []
    for i in range(TM // blk):
        rows = slice(i * blk, (i + 1) * blk)
        parts.append(_dot(tri, la_hi[rows]) + _dot(tri, la_mid[rows]) + _dot(tri, la_lo[rows]))
    b = jnp.concatenate(parts, axis=0)

    q = proj(C_GQ, GLA_QK)
    k = proj(C_GK, GLA_QK)
    v = proj(C_GV, GLA_V).astype(bf16)
    qe = (q * (GLA_DK ** -0.5) * jnp.exp(b)).astype(bf16)
    ke = (k * jnp.exp(-b)).astype(bf16)

    C = GLA_CHUNK
    lane_q = lax.broadcasted_iota(jnp.int32, (C, GLA_QK), 1) // GLA_DK
    row_a = lax.broadcasted_iota(jnp.int32, (GLA_HEADS * C, C), 0) % C
    col_a = lax.broadcasted_iota(jnp.int32, (GLA_HEADS * C, C), 1)
    causal = row_a >= col_a
    lane_s = lax.broadcasted_iota(jnp.int32, (GLA_DV, GLA_QK), 1) // GLA_DK
    for ci in range(TM // C):
        rows = slice(ci * C, (ci + 1) * C)
        bl = b[ci * C + C - 1:ci * C + C, :]
        kl = (k[rows] * jnp.exp(bl - b[rows])).astype(bf16)
        qe_c = qe[rows]
        zero = jnp.zeros_like(qe_c)
        qm = jnp.concatenate([jnp.where(lane_q == hh, qe_c, zero) for hh in range(GLA_HEADS)], axis=0)
        a = lax.dot_general(qm, ke[rows], _NT, preferred_element_type=f32)
        a = jnp.where(causal, a, 0.0).astype(bf16)
        v_c = v[rows]
        o_intra = _dot(a, v_c)
        st = st_ref[...]
        o_inter = lax.dot_general(qm, st.astype(bf16), _NT, preferred_element_type=f32)
        o_c = jnp.concatenate(
            [o_intra[hh * C:(hh + 1) * C, hh * GLA_DV:(hh + 1) * GLA_DV] + o_inter[hh * C:(hh + 1) * C]
             for hh in range(GLA_HEADS)], axis=1)
        og_ref[rows, :] = o_c
        up = lax.dot_general(v_c, kl, _TN, preferred_element_type=f32)
        new = up[0:GLA_DV]
        for hh in range(1, GLA_HEADS):
            new = jnp.where(lane_s == hh, up[hh * GLA_DV:(hh + 1) * GLA_DV], new)
        st_ref[...] = st * jnp.exp(bl) + new

    og = og_ref[...]
    gg = proj(C_GG, GLA_V)
    gn = gnorm_ref[...]
    heads = []
    for hh in range(GLA_HEADS):
        cols = slice(hh * GLA_DV, (hh + 1) * GLA_DV)
        heads.append(_rmsnorm(og[:, cols], gn))
    o_gla = jnp.concatenate(heads, axis=1) * (gg / (1.0 + jnp.exp(-gg)))

    pos = pos_ref[0].astype(f32)
    ang = pos * rope_ref[0:1, :]
    cs = jnp.cos(ang)
    sn = jnp.sin(ang)
    rc = 1.0 + rope_ref[1:2, :] * (cs - 1.0)
    rs1 = rope_ref[2:3, :] * sn
    rs2 = rope_ref[3:4, :] * sn
    half = ROPE_DIM // 2

    def rot(xg):
        return xg * rc + pltpu.roll(xg, LANES - half, 1) * rs1 + pltpu.roll(xg, half, 1) * rs2

    sq = proj(C_SQ, SWA_Q)
    scale = SWA_HD ** -0.5
    qr = [(rot(sq[:, p * LANES:(p + 1) * LANES]) * scale).astype(bf16) for p in range(SWA_GROUP)]
    kbuf[SWA_BLOCK:, :] = rot(proj(C_SK, SWA_KV)).astype(bf16)
    vbuf[SWA_BLOCK:, :] = proj(C_SV, SWA_KV).astype(bf16)

    W = SWA_BLOCK
    row_s = lax.broadcasted_iota(jnp.int32, (2 * W, 2 * W), 0) % W
    col_s = lax.broadcasted_iota(jnp.int32, (2 * W, 2 * W), 1)
    band = (col_s > row_s) & (col_s <= row_s + W)
    lane_p = lax.broadcasted_iota(jnp.int32, (W, LANES), 1)
    row_h = lax.broadcasted_iota(jnp.int32, (2 * W, 1), 0)
    for j in range(TM // W):
        kk = kbuf[j * W:j * W + 2 * W, :]
        vv = vbuf[j * W:j * W + 2 * W, :]
        mask = band
        if j == 0:
            mask = band & ((col_s >= W) | (t > 0))
        for p in range(SWA_GROUP):
            qp = qr[p][j * W:(j + 1) * W]
            zq = jnp.zeros_like(qp)
            qm = jnp.concatenate([jnp.where(lane_p < SWA_HD, qp, zq),
                                  jnp.where(lane_p >= SWA_HD, qp, zq)], axis=0)
            s = lax.dot_general(qm, kk, _NT, preferred_element_type=f32)
            s = jnp.where(mask, s, -jnp.inf)
            sink = jnp.where(row_h < W, sinks_ref[p], sinks_ref[SWA_GROUP + p])
            m = jnp.maximum(jnp.max(s, axis=1, keepdims=True), sink)
            pe = jnp.exp(s - m)
            den = jnp.sum(pe, axis=1, keepdims=True) + jnp.exp(sink - m)
            o = _dot(pe.astype(bf16), vv) * (1.0 / den)
            os_ref[j * W:(j + 1) * W, p * LANES:(p + 1) * LANES] = jnp.where(
                lane_p < SWA_HD, o[0:W], o[W:2 * W])
    kbuf[0:W, :] = kbuf[TM:TM + W, :]
    vbuf[0:W, :] = vbuf[TM:TM + W, :]

    mix = _dot(o_gla.astype(bf16), wout_ref[0:GLA_V, :]) + \
        _dot(os_ref[...].astype(bf16), wout_ref[GLA_V:GLA_V + SWA_Q, :])
    o_ref[0] = x + _rmsnorm(mix, gpost_ref[...])


def _ffn_kernel(x_ref, gpre_ref, wv_ref, wg_ref, cwv_ref, cwg_ref, cbv_ref, cbg_ref,
                wd_ref, gpost_ref, o_ref, hbuf, uv_ref, ug_ref, act_ref):
    f32, bf16 = jnp.float32, jnp.bfloat16
    t = pl.program_id(1)
    PAD = BF16_ROWS

    @pl.when(t == 0)
    def _():
        hbuf[0:PAD, :] = jnp.zeros((PAD, D_MODEL), bf16)

    x = x_ref[0]
    hbuf[PAD:, :] = _rmsnorm(x, gpre_ref[...]).astype(bf16)

    def conv(u_ref, r0, w, bias):
        lo = PAD + r0
        return (bias + w[2:3] * u_ref[lo:lo + FF_ROWS, :] + w[1:2] * u_ref[lo - 1:lo - 1 + FF_ROWS, :]
                + w[0:1] * u_ref[lo - 2:lo - 2 + FF_ROWS, :])

    for j in range(N_FF_TILES):
        hext = hbuf[...]
        uv = uv_ref.at[j % U_BUFS]
        ug = ug_ref.at[j % U_BUFS]
        uv[...] = _dot(hext, wv_ref[j])
        ug[...] = _dot(hext, wg_ref[j])
        cwv, cbv, cwg, cbg = cwv_ref[j], cbv_ref[j], cwg_ref[j], cbg_ref[j]
        for r0 in range(0, TM, FF_ROWS):
            val = conv(uv, r0, cwv, cbv)
            gate = conv(ug, r0, cwg, cbg)
            poly = (GELU_C * 0.044715) * (gate * gate) + GELU_C
            half_gate = 0.5 * gate
            act = (half_gate + half_gate * jnp.tanh(gate * poly)) * val
            act_ref[r0:r0 + FF_ROWS, j * FF_TILE:(j + 1) * FF_TILE] = act.astype(bf16)

    y = _dot(act_ref[...], wd_ref[...])
    hbuf[0:PAD, :] = hbuf[TM:TM + PAD, :]
    o_ref[0] = x + _rmsnorm(y, gpost_ref[...])


def _const_spec(shape):
    nd = len(shape)
    return pl.BlockSpec(shape, lambda b, t: (0,) * nd)


def _layer(x, pos3, pre_mix_norm, w_in, gla_gate_up, gla_gate_bias, gla_out_norm, swa_sinks,
           w_out, post_mix_norm, pre_ffn_norm, w_up, conv_w, conv_b, w_down, post_ffn_norm):
    f32, bf16 = jnp.float32, jnp.bfloat16
    B, T, _ = x.shape
    grid = (B, T // TM)

    gq, gk, gv, glr, gg, sq, sk, sv = jnp.split(
        w_in, np.cumsum((GLA_QK, GLA_QK, GLA_V, GLA_GATE_RANK, GLA_V, SWA_Q, SWA_KV)).tolist(), axis=1)
    perm = np.array([[p, p + SWA_GROUP] for p in range(SWA_GROUP)]).reshape(-1)
    sq = sq.reshape(D_MODEL, SWA_HEADS, SWA_HD)[:, perm].reshape(D_MODEL, SWA_Q)
    glr = jnp.pad(glr, ((0, 0), (0, LANES - GLA_GATE_RANK)))
    win = jnp.concatenate([gq, gk, gv, gg, sq, sk, sv, glr], axis=1).astype(bf16)
    gup = jnp.pad(gla_gate_up, ((0, LANES - GLA_GATE_RANK), (0, 0))).astype(bf16)
    wo_swa = w_out[GLA_V:].reshape(SWA_HEADS, SWA_HD, D_MODEL)[perm].reshape(SWA_Q, D_MODEL)
    wout = jnp.concatenate([w_out[:GLA_V], wo_swa], axis=0).astype(bf16)

    d = np.arange(LANES) % SWA_HD
    half = ROPE_DIM // 2
    inv_freq = ROPE_THETA ** (-jnp.arange(half, dtype=f32) * (2.0 / ROPE_DIM))
    rope = jnp.zeros((8, LANES), f32)
    rope = rope.at[0].set(jnp.where(d < ROPE_DIM, inv_freq[d % half], 0.0))
    rope = rope.at[1].set(jnp.asarray(d < ROPE_DIM, f32))
    rope = rope.at[2].set(jnp.asarray(np.where(d < half, -1.0, 0.0), f32))
    rope = rope.at[3].set(jnp.asarray(np.where((d >= half) & (d < ROPE_DIM), 1.0, 0.0), f32))

    x1 = pl.pallas_call(
        _mix_kernel,
        grid=grid,
        in_specs=[
            pl.BlockSpec(memory_space=pltpu.SMEM),
            pl.BlockSpec((1, TM, D_MODEL), lambda b, t: (b, t, 0)),
            pl.BlockSpec((1, TM, 1), lambda b, t: (b, t, 0)),
            _const_spec((1, D_MODEL)),
            _const_spec((D_MODEL, IN_COLS)),
            _const_spec((LANES, GLA_QK)),
            _const_spec((1, GLA_QK)),
            _const_spec((1, GLA_DV)),
            _const_spec((GLA_V + SWA_Q, D_MODEL)),
            _const_spec((1, D_MODEL)),
            _const_spec((8, LANES)),
        ],
        out_specs=pl.BlockSpec((1, TM, D_MODEL), lambda b, t: (b, t, 0)),
        out_shape=jax.ShapeDtypeStruct(x.shape, f32),
        scratch_shapes=[
            pltpu.VMEM((GLA_DV, GLA_QK), f32),
            pltpu.VMEM((TM + SWA_BLOCK, SWA_KV), bf16),
            pltpu.VMEM((TM + SWA_BLOCK, SWA_KV), bf16),
            pltpu.VMEM((TM, GLA_V), f32),
            pltpu.VMEM((TM, SWA_Q), f32),
        ],
        compiler_params=pltpu.CompilerParams(
            dimension_semantics=("arbitrary", "arbitrary"), vmem_limit_bytes=VMEM_LIMIT),
    )(swa_sinks, x, pos3, pre_mix_norm.reshape(1, D_MODEL), win, gup,
      gla_gate_bias.reshape(1, GLA_QK), gla_out_norm.reshape(1, GLA_DV), wout,
      post_mix_norm.reshape(1, D_MODEL), rope)

    def tiles(w):
        return w.reshape(w.shape[0], N_FF_TILES, FF_TILE).transpose(1, 0, 2)

    wv = tiles(w_up[:, :D_FF]).astype(bf16)
    wg = tiles(w_up[:, D_FF:]).astype(bf16)
    cwv = tiles(conv_w[:, :D_FF])
    cwg = tiles(conv_w[:, D_FF:])
    cbv = tiles(conv_b[None, :D_FF])
    cbg = tiles(conv_b[None, D_FF:])
    wd = w_down.astype(bf16)

    def const3(shape):
        return pl.BlockSpec(shape, lambda b, t: (0, 0, 0), pipeline_mode=pl.Buffered(1))

    out = pl.pallas_call(
        _ffn_kernel,
        grid=grid,
        in_specs=[
            pl.BlockSpec((1, TM, D_MODEL), lambda b, t: (b, t, 0)),
            _const_spec((1, D_MODEL)),
            const3((N_FF_TILES, D_MODEL, FF_TILE)),
            const3((N_FF_TILES, D_MODEL, FF_TILE)),
            const3((N_FF_TILES, CONV_WIDTH, FF_TILE)),
            const3((N_FF_TILES, CONV_WIDTH, FF_TILE)),
            const3((N_FF_TILES, 1, FF_TILE)),
            const3((N_FF_TILES, 1, FF_TILE)),
            pl.BlockSpec((D_FF, D_MODEL), lambda b, t: (0, 0), pipeline_mode=pl.Buffered(1)),
            _const_spec((1, D_MODEL)),
        ],
        out_specs=pl.BlockSpec((1, TM, D_MODEL), lambda b, t: (b, t, 0)),
        out_shape=jax.ShapeDtypeStruct(x.shape, f32),
        scratch_shapes=[
            pltpu.VMEM((TM + BF16_ROWS, D_MODEL), bf16),
            pltpu.VMEM((U_BUFS, TM + BF16_ROWS, FF_TILE), f32),
            pltpu.VMEM((U_BUFS, TM + BF16_ROWS, FF_TILE), f32),
            pltpu.VMEM((TM, D_FF), bf16),
        ],
        compiler_params=pltpu.CompilerParams(
            dimension_semantics=("arbitrary", "arbitrary"), vmem_limit_bytes=VMEM_LIMIT),
    )(x1, pre_ffn_norm.reshape(1, D_MODEL), wv, wg, cwv, cwg, cbv, cbg, wd,
      post_ffn_norm.reshape(1, D_MODEL))
    return out


def kernel(x, positions, pre_mix_norm, w_in, gla_gate_up, gla_gate_bias, gla_out_norm, swa_sinks,
           w_out, post_mix_norm, pre_ffn_norm, w_up, conv_w, conv_b, w_down, post_ffn_norm):
    B, T, _ = x.shape
    pos3 = positions.reshape(B, T, 1)
    for l in range(pre_mix_norm.shape[0]):
        x = _layer(x, pos3, pre_mix_norm[l], w_in[l], gla_gate_up[l], gla_gate_bias[l],
                   gla_out_norm[l], swa_sinks[l], w_out[l], post_mix_norm[l], pre_ffn_norm[l],
                   w_up[l], conv_w[l], conv_b[l], w_down[l], post_ffn_norm[l])
    return x
```
